```python
import jax, jax.numpy as jnp
from jax import lax
import numpy as np

D_MODEL = 1024
BATCH = 8
SEQ = 2048
DEPTH = 2
DEC_BATCH = 128
DEC_SEQ = 1
PAST_LEN = 16384
PAGE_SIZE = 128

N_BRANCH = 4
BR_WIDTH = D_MODEL // 2
N_GROUPS = 4
GROUP_CH = BR_WIDTH // N_GROUPS
CHUNK = 128
CONV_B_WIDTH = 3
POOL_WINDOWS = (2, 4, 8, 16)
POOL_STATE = max(POOL_WINDOWS) - 1
CONF_WIDTH = 31
N_EXPERTS = 64
TOP_K = 8
N_EXPERT_GROUPS = 8
TOPK_GROUPS = 4
D_EXPERT = D_MODEL // 4
D_SHARED = D_EXPERT
ROUTED_SCALE = 2.5
EPS = 1e-6

A_COLS = 2 * BR_WIDTH
B_COLS = 3 * BR_WIDTH
C_COLS = BR_WIDTH
D_COLS = 2 * BR_WIDTH
G_COLS = N_BRANCH * D_MODEL
IN_COLS = A_COLS + B_COLS + C_COLS + D_COLS + G_COLS
IN_SPLITS = (A_COLS, A_COLS + B_COLS, A_COLS + B_COLS + C_COLS, A_COLS + B_COLS + C_COLS + D_COLS)

kernel_name = "hybrid_gated_conv_pool_chunkmlp_moe_decoder_step"


def rmsnorm(x, g):
    x32 = x.astype(jnp.float32)
    y = x32 * lax.rsqrt(jnp.mean(x32 * x32, axis=-1, keepdims=True) + EPS)
    return (y * g.astype(jnp.float32)).astype(x.dtype)


def layernorm(x, g, b):
    x32 = x.astype(jnp.float32)
    mu = jnp.mean(x32, axis=-1, keepdims=True)
    xc = x32 - mu
    var = jnp.mean(xc * xc, axis=-1, keepdims=True)
    return (xc * lax.rsqrt(var + EPS) * g.astype(jnp.float32) + b.astype(jnp.float32)).astype(x.dtype)


def causal_depthwise_conv(x, prev, w):
    xp = jnp.concatenate([prev.astype(x.dtype), x], axis=1)
    y = lax.conv_general_dilated(xp, w[:, None, :].astype(x.dtype), (1,), 'VALID',
                                 dimension_numbers=('NWC', 'WIO', 'NWC'),
                                 feature_group_count=x.shape[-1])
    return y, xp[:, -(w.shape[0] - 1):]


def chunk_spatial_gate(v, w_s, b_s):
    bsz, t, c = v.shape
    n_chunks = -(-t // CHUNK)
    pad = n_chunks * CHUNK - t
    vp = jnp.pad(v, ((0, 0), (0, pad), (0, 0))).reshape(bsz, n_chunks, CHUNK, N_GROUPS, GROUP_CH)
    mask = jnp.tril(jnp.ones((CHUNK, CHUNK), dtype=bool))
    w = jnp.where(mask[None], w_s, 0).astype(v.dtype)
    s = jnp.einsum('gpq,bnqgc->bnpgc', w, vp) + jnp.transpose(b_s)[None, None, :, :, None].astype(v.dtype)
    return s.reshape(bsz, n_chunks * CHUNK, c)[:, :t]


def multi_scale_pool(p, prev, start_pos):
    bsz, t, c = p.shape
    xp = jnp.concatenate([prev.astype(p.dtype), p], axis=1)
    cs = jnp.pad(jnp.cumsum(xp.astype(jnp.float32), axis=1), ((0, 0), (1, 0), (0, 0)))
    end = cs[:, POOL_STATE + 1:]
    pos = start_pos + jnp.arange(t)
    outs = []
    for g, win in enumerate(POOL_WINDOWS):
        sl = slice(g * GROUP_CH, (g + 1) * GROUP_CH)
        s0 = POOL_STATE + 1 - win
        wsum = end[:, :, sl] - cs[:, s0:s0 + t, sl]
        cnt = jnp.minimum(win, pos + 1).astype(jnp.float32)[None, :, None]
        outs.append(wsum / cnt)
    mean = jnp.concatenate(outs, axis=-1)
    return (mean - p.astype(jnp.float32)).astype(p.dtype), xp[:, -POOL_STATE:]


def token_mixer(h, conv_prev, pool_prev, conf_prev, start_pos, w_in, sgu_ln_g, sgu_ln_b, sgu_w, sgu_b,
                conv_w, pool_w, pool_scale, conf_w, conf_b, conf_ln_g, conf_ln_b, w_branch, w_out):
    bsz, t, _ = h.shape
    z = h @ w_in
    za, zb, zc, zd, zg = jnp.split(z, IN_SPLITS, axis=-1)
    u, v = jnp.split(jax.nn.gelu(za, approximate=False), 2, axis=-1)
    v = layernorm(v, sgu_ln_g, sgu_ln_b)
    ya = u * chunk_spatial_gate(v, sgu_w, sgu_b)
    gate_b, gate_c, hb = jnp.split(zb, 3, axis=-1)
    conv_out, conv_state = causal_depthwise_conv(gate_c * hb, conv_prev, conv_w)
    yb = gate_b * conv_out
    pooled, pool_state = multi_scale_pool(zc, pool_prev, start_pos)
    yc = jnp.einsum('btgc,gcd->btgd', pooled.reshape(bsz, t, N_GROUPS, GROUP_CH), pool_w)
    yc = yc.reshape(bsz, t, BR_WIDTH) * pool_scale
    a, b = jnp.split(zd, 2, axis=-1)
    glu = a * jax.nn.sigmoid(b)
    dconv, conf_state = causal_depthwise_conv(glu, conf_prev, conf_w)
    yd = jax.nn.silu(layernorm(dconv + conf_b, conf_ln_g, conf_ln_b))
    branches = jnp.stack([ya, yb, yc, yd], axis=2)
    proj = jnp.einsum('btkc,kcd->btkd', branches, w_branch)
    gates = jax.nn.sigmoid(zg.reshape(bsz, t, N_BRANCH, D_MODEL))
    merged = jnp.sum(gates * proj, axis=2)
    return merged @ w_out, (conv_state, pool_state, conf_state, v)


def moe_ffn(h, router_w, router_bias, e_gate, e_up, e_down, s_gate, s_up, s_down):
    n = h.shape[0]
    scores = jax.nn.sigmoid((h @ router_w).astype(jnp.float32))
    biased = scores + router_bias.astype(jnp.float32)
    grp = biased.reshape(n, N_EXPERT_GROUPS, N_EXPERTS // N_EXPERT_GROUPS)
    grp_score = jnp.sum(lax.top_k(grp, 2)[0], axis=-1)
    _, grp_idx = lax.top_k(grp_score, TOPK_GROUPS)
    grp_mask = jnp.sum(jax.nn.one_hot(grp_idx, N_EXPERT_GROUPS, dtype=jnp.float32), axis=1) > 0
    expert_mask = jnp.repeat(grp_mask, N_EXPERTS // N_EXPERT_GROUPS, axis=1)
    _, idx = lax.top_k(jnp.where(expert_mask, biased, -jnp.inf), TOP_K)
    w = jnp.take_along_axis(scores, idx, axis=1)
    w = w / jnp.sum(w, axis=-1, keepdims=True) * ROUTED_SCALE
    gates = jnp.zeros((n, N_EXPERTS), jnp.float32).at[jnp.arange(n)[:, None], idx].set(w)
    hg = jnp.einsum('nd,edf->nef', h, e_gate)
    hu = jnp.einsum('nd,edf->nef', h, e_up)
    act = jax.nn.silu(hg) * hu * gates[:, :, None].astype(h.dtype)
    routed = jnp.einsum('nef,efd->nd', act, e_down)
    shared = (jax.nn.silu(h @ s_gate) * (h @ s_up)) @ s_down
    return routed + shared


def decoder_layer(x, c, conv_prev, pool_prev, conf_prev, start_pos, per_sequence_moe,
                  ada_w, ada_b, norm1_g, norm2_g, w_in, sgu_ln_g, sgu_ln_b, sgu_w, sgu_b,
                  conv_w, pool_w, pool_scale, conf_w, conf_b, conf_ln_g, conf_ln_b,
                  w_branch, w_out, router_w, router_bias, e_gate, e_up, e_down,
                  s_gate, s_up, s_down):
    mod = (jax.nn.silu(c) @ ada_w + ada_b)[:, None, :]
    shift1, scale1, gate1, shift2, scale2, gate2 = jnp.split(mod, 6, axis=-1)
    h = rmsnorm(x, norm1_g) * (1 + scale1) + shift1
    mix, new_states = token_mixer(h, conv_prev, pool_prev, conf_prev, start_pos, w_in, sgu_ln_g, sgu_ln_b,
                                  sgu_w, sgu_b, conv_w, pool_w, pool_scale, conf_w, conf_b,
                                  conf_ln_g, conf_ln_b, w_branch, w_out)
    x = x + gate1 * mix
    h = rmsnorm(x, norm2_g) * (1 + scale2) + shift2
    moe = lambda tok: moe_ffn(tok, router_w, router_bias, e_gate, e_up, e_down, s_gate, s_up, s_down)
    if per_sequence_moe:
        f = lax.map(moe, h)
    else:
        f = moe(h.reshape(-1, D_MODEL)).reshape(h.shape)
    x = x + gate2 * f
    return x, new_states


def setup_inputs(seed: int = 0) -> dict:
    key = jax.random.key(seed)
    ks = iter(jax.random.split(key, 48))
    f32 = jnp.float32

    def nrm(shape, scale):
        return jax.random.normal(next(ks), shape, f32) * scale

    def gain(shape):
        return 1.0 + nrm(shape, 0.05)

    L = DEPTH
    return {
        "x_prompt": nrm((BATCH, SEQ, D_MODEL), 1.0),
        "x_sample": nrm((DEC_BATCH, DEC_SEQ, D_MODEL), 1.0),
        "state_conv": nrm((L, DEC_BATCH, CONV_B_WIDTH - 1, BR_WIDTH), 1.0),
        "state_pool": nrm((L, DEC_BATCH, POOL_STATE, BR_WIDTH), 1.0),
        "state_conf": nrm((L, DEC_BATCH, CONF_WIDTH - 1, BR_WIDTH), 0.5),
        "c_prompt": nrm((BATCH, D_MODEL), 1.0),
        "c_sample": nrm((DEC_BATCH, D_MODEL), 1.0),
        "ada_w": nrm((L, D_MODEL, 6 * D_MODEL), 0.5 * D_MODEL ** -0.5),
        "ada_b": nrm((L, 6 * D_MODEL), 0.02),
        "norm1_g": gain((L, D_MODEL)),
        "norm2_g": gain((L, D_MODEL)),
        "w_in": nrm((L, D_MODEL, IN_COLS), D_MODEL ** -0.5),
        "sgu_ln_g": gain((L, BR_WIDTH)),
        "sgu_ln_b": nrm((L, BR_WIDTH), 0.02),
        "sgu_w": nrm((L, N_GROUPS, CHUNK, CHUNK), CHUNK ** -0.5),
        "sgu_b": 1.0 + nrm((L, N_GROUPS, CHUNK), 0.02),
        "conv_w": nrm((L, CONV_B_WIDTH, BR_WIDTH), CONV_B_WIDTH ** -0.5),
        "pool_w": nrm((L, N_GROUPS, GROUP_CH, GROUP_CH), GROUP_CH ** -0.5),
        "pool_scale": 1.0 + nrm((L, BR_WIDTH), 0.1),
        "conf_w": nrm((L, CONF_WIDTH, BR_WIDTH), CONF_WIDTH ** -0.5),
        "conf_b": nrm((L, BR_WIDTH), 0.02),
        "conf_ln_g": gain((L, BR_WIDTH)),
        "conf_ln_b": nrm((L, BR_WIDTH), 0.02),
        "w_branch": nrm((L, N_BRANCH, BR_WIDTH, D_MODEL), BR_WIDTH ** -0.5),
        "w_out": nrm((L, D_MODEL, D_MODEL), D_MODEL ** -0.5),
        "router_w": nrm((L, D_MODEL, N_EXPERTS), D_MODEL ** -0.5),
        "router_bias": nrm((L, N_EXPERTS), 0.01),
        "e_gate": nrm((L, N_EXPERTS, D_MODEL, D_EXPERT), D_MODEL ** -0.5),
        "e_up": nrm((L, N_EXPERTS, D_MODEL, D_EXPERT), D_MODEL ** -0.5),
        "e_down": nrm((L, N_EXPERTS, D_EXPERT, D_MODEL), D_EXPERT ** -0.5),
        "s_gate": nrm((L, D_MODEL, D_SHARED), D_MODEL ** -0.5),
        "s_up": nrm((L, D_MODEL, D_SHARED), D_MODEL ** -0.5),
        "s_down": nrm((L, D_SHARED, D_MODEL), D_SHARED ** -0.5),
        "final_g": gain((D_MODEL,)),
    }


def reference(x_prompt, x_sample, state_conv, state_pool, state_conf, c_prompt, c_sample,
              ada_w, ada_b, norm1_g, norm2_g, w_in, sgu_ln_g, sgu_ln_b, sgu_w, sgu_b,
              conv_w, pool_w, pool_scale, conf_w, conf_b, conf_ln_g, conf_ln_b,
              w_branch, w_out, router_w, router_bias, e_gate, e_up, e_down,
              s_gate, s_up, s_down, final_g):
    bp = x_prompt.shape[0]
    dt = x_prompt.dtype
    yp, ys = x_prompt, x_sample
    conv_p, pool_p, conf_p = [], [], []
    conv_s, pool_s, conf_s, chunk_v_s = [], [], [], []
    for l in range(DEPTH):
        lw = (ada_w[l], ada_b[l], norm1_g[l], norm2_g[l], w_in[l], sgu_ln_g[l], sgu_ln_b[l], sgu_w[l], sgu_b[l],
              conv_w[l], pool_w[l], pool_scale[l], conf_w[l], conf_b[l], conf_ln_g[l], conf_ln_b[l],
              w_branch[l], w_out[l], router_w[l], router_bias[l], e_gate[l], e_up[l], e_down[l],
              s_gate[l], s_up[l], s_down[l])
        yp, (cp, pp, fp, _) = decoder_layer(
            yp, c_prompt,
            jnp.zeros((bp, CONV_B_WIDTH - 1, BR_WIDTH), dt),
            jnp.zeros((bp, POOL_STATE, BR_WIDTH), dt),
            jnp.zeros((bp, CONF_WIDTH - 1, BR_WIDTH), dt),
            0, True, *lw)
        ys, (cs, ps, fs, vs) = decoder_layer(
            ys, c_sample, state_conv[l], state_pool[l], state_conf[l], PAST_LEN, False, *lw)
        conv_p.append(cp); pool_p.append(pp); conf_p.append(fp)
        conv_s.append(cs); pool_s.append(ps); conf_s.append(fs); chunk_v_s.append(vs)
    y_prompt = rmsnorm(yp, final_g)
    y_sample = rmsnorm(ys, final_g)
    return (y_prompt, y_sample,
            jnp.stack(conv_p), jnp.stack(pool_p), jnp.stack(conf_p),
            jnp.stack(conv_s), jnp.stack(pool_s), jnp.stack(conf_s), jnp.stack(chunk_v_s))
```

```python
import functools
import math

import jax
import jax.numpy as jnp
from jax import lax
from jax.experimental import pallas as pl
from jax.experimental.pallas import tpu as pltpu

D_MODEL = 1024
BR_WIDTH = 512
N_GROUPS = 4
GROUP_CH = 128
CHUNK = 128
CONV_B_WIDTH = 3
POOL_WINDOWS = (2, 4, 8, 16)
POOL_STATE = 15
CONF_WIDTH = 31
N_EXPERTS = 64
TOP_K = 8
N_EXPERT_GROUPS = 8
GROUP_SIZE = N_EXPERTS // N_EXPERT_GROUPS
TOPK_GROUPS = 4
D_EXPERT = 256
ROUTED_SCALE = 2.5
EPS = 1e-6
PAST_LEN = 16384
IN_COLS = 8192

COL_A = 0
COL_B = 1024
COL_C = 2560
COL_D = 3072
COL_G = 4096

SUBLANES = 8
CONV_HIST = 8
POOL_HIST = 16
CONF_HIST = 32

VMEM_LIMIT_BYTES = 56 * 1024 * 1024

F32 = jnp.float32
BF16 = jnp.bfloat16


def _sigmoid(x):
    return 1.0 / (1.0 + jnp.exp(-x))


def _silu(x):
    return x * _sigmoid(x)


def _gelu(x):
    return 0.5 * x * (1.0 + lax.erf(x * (1.0 / math.sqrt(2.0))))


def _dot(a, b):
    return jnp.dot(a.astype(BF16), b, preferred_element_type=F32)


def _rms_mod(x, g, scale, shift):
    ms = jnp.mean(x * x, axis=-1, keepdims=True)
    return x * lax.rsqrt(ms + EPS) * g * (1.0 + scale) + shift


def _layernorm(x, g, b):
    mu = jnp.mean(x, axis=-1, keepdims=True)
    xc = x - mu
    var = jnp.mean(xc * xc, axis=-1, keepdims=True)
    return xc * lax.rsqrt(var + EPS) * g + b


def _mod_kernel(c_ref, w_ref, b_ref, o_ref):
    c = c_ref[...]
    o_ref[...] = _dot(_silu(c), w_ref[...].astype(BF16)) + b_ref[...]


def _modulation(c_all, ada_w, ada_b):
    n_layers = ada_w.shape[0]
    rows = c_all.shape[0]
    tn = 1536
    return pl.pallas_call(
        _mod_kernel,
        grid=(n_layers, 6 * D_MODEL // tn),
        in_specs=[
            pl.BlockSpec((rows, D_MODEL), lambda l, j: (0, 0)),
            pl.BlockSpec((None, D_MODEL, tn), lambda l, j: (l, 0, j)),
            pl.BlockSpec((None, 1, tn), lambda l, j: (l, 0, j)),
        ],
        out_specs=pl.BlockSpec((None, rows, tn), lambda l, j: (l, 0, j)),
        out_shape=jax.ShapeDtypeStruct((n_layers, rows, 6 * D_MODEL), F32),
        compiler_params=pltpu.CompilerParams(
            dimension_semantics=("arbitrary", "arbitrary"), vmem_limit_bytes=VMEM_LIMIT_BYTES),
        name="adaln_modulation",
    )(c_all, ada_w, ada_b.reshape(n_layers, 1, 6 * D_MODEL))


def _merge_branches(h16, ys, win_ref, wbr_ref, wout_ref):
    merged = None
    for k, y in enumerate(ys):
        zg = jnp.dot(h16, win_ref[:, COL_G + k * D_MODEL:COL_G + (k + 1) * D_MODEL],
                     preferred_element_type=F32)
        term = _sigmoid(zg) * _dot(y, wbr_ref[k])
        merged = term if merged is None else merged + term
    return _dot(merged, wout_ref[...])


def _tril_weights(sguw_ref):
    row = lax.broadcasted_iota(jnp.int32, (CHUNK, CHUNK), 0)
    col = lax.broadcasted_iota(jnp.int32, (CHUNK, CHUNK), 1)
    keep = col <= row
    return [jnp.where(keep, sguw_ref[g], 0.0).astype(BF16) for g in range(N_GROUPS)]


def _mixer_prompt_kernel(x_ref, mod_ref, n1g_ref, win_ref, lng_ref, lnb_ref, sguw_ref, sgub_ref,
                         convw_ref, poolw_ref, pscale_ref, confw_ref, confb_ref, clng_ref, clnb_ref,
                         wbr_ref, wout_ref,
                         xo_ref, cst_ref, pst_ref, fst_ref,
                         cbuf, pbuf, gbuf, dbuf, sbuf, *, tm):
    t = pl.program_id(1)
    last_t = pl.num_programs(1) - 1

    @pl.when(t == 0)
    def _():
        cbuf[0:CONV_HIST, :] = jnp.zeros((CONV_HIST, BR_WIDTH), F32)
        pbuf[0:POOL_HIST, :] = jnp.zeros((POOL_HIST, BR_WIDTH), F32)
        gbuf[0:CONF_HIST, :] = jnp.zeros((CONF_HIST, BR_WIDTH), F32)

    x = x_ref[...]
    shift1 = mod_ref[0:1, :]
    scale1 = mod_ref[1:2, :]
    gate1 = mod_ref[2:3, :]
    h16 = _rms_mod(x, n1g_ref[...], scale1, shift1).astype(BF16)

    za = jnp.dot(h16, win_ref[:, COL_A:COL_A + 2 * BR_WIDTH], preferred_element_type=F32)
    ga = _gelu(za)
    u = ga[:, :BR_WIDTH]
    v16 = _layernorm(ga[:, BR_WIDTH:], lng_ref[...], lnb_ref[...]).astype(BF16)
    n_chunks = tm // CHUNK
    wtril = _tril_weights(sguw_ref)
    s_cols = []
    for g in range(N_GROUPS):
        vcat = jnp.concatenate(
            [v16[c * CHUNK:(c + 1) * CHUNK, g * GROUP_CH:(g + 1) * GROUP_CH] for c in range(n_chunks)],
            axis=1)
        sg = jnp.dot(wtril[g], vcat, preferred_element_type=F32) + sgub_ref[g]
        s_cols.append(jnp.concatenate(
            [sg[:, c * GROUP_CH:(c + 1) * GROUP_CH] for c in range(n_chunks)], axis=0))
    ya = u * jnp.concatenate(s_cols, axis=1)

    zb = jnp.dot(h16, win_ref[:, COL_B:COL_B + 3 * BR_WIDTH], preferred_element_type=F32)
    gate_b = zb[:, :BR_WIDTH]
    cbuf[CONV_HIST:CONV_HIST + tm, :] = zb[:, BR_WIDTH:2 * BR_WIDTH] * zb[:, 2 * BR_WIDTH:]
    conv = None
    for j in range(CONV_B_WIDTH):
        off = CONV_HIST - (CONV_B_WIDTH - 1) + j
        term = convw_ref[j:j + 1, :] * cbuf[off:off + tm, :]
        conv = term if conv is None else conv + term
    yb = gate_b * conv

    p = jnp.dot(h16, win_ref[:, COL_C:COL_C + BR_WIDTH], preferred_element_type=F32)
    pbuf[POOL_HIST:POOL_HIST + tm, :] = p
    pos1 = t * tm + lax.broadcasted_iota(jnp.int32, (tm, 1), 0) + 1
    yc_cols = []
    for g, win in enumerate(POOL_WINDOWS):
        lanes = slice(g * GROUP_CH, (g + 1) * GROUP_CH)
        wsum = None
        for i in range(win):
            term = pbuf[POOL_HIST - i:POOL_HIST - i + tm, lanes]
            wsum = term if wsum is None else wsum + term
        cnt = jnp.minimum(win, pos1).astype(F32)
        pooled = wsum / cnt - p[:, lanes]
        yc_cols.append(_dot(pooled, poolw_ref[g]))
    yc = jnp.concatenate(yc_cols, axis=1) * pscale_ref[...]

    zd = jnp.dot(h16, win_ref[:, COL_D:COL_D + 2 * BR_WIDTH], preferred_element_type=F32)
    gbuf[CONF_HIST:CONF_HIST + tm, :] = zd[:, :BR_WIDTH] * _sigmoid(zd[:, BR_WIDTH:])
    span = tm + CONF_HIST - SUBLANES
    for s in range(1, SUBLANES):
        sbuf[s - 1, 0:span, :] = gbuf[s:s + span, :]
    rows = 32

    def conv_rows(r, carry):
        base = pl.multiple_of(r * rows, rows)
        acc = None
        for j in range(CONF_WIDTH):
            off = CONF_HIST - (CONF_WIDTH - 1) + j
            s = off % SUBLANES
            start = pl.multiple_of(base + (off - s), SUBLANES)
            win = gbuf[pl.ds(start, rows), :] if s == 0 else sbuf[s - 1, pl.ds(start, rows), :]
            term = confw_ref[j:j + 1, :] * win
            acc = term if acc is None else acc + term
        dbuf[pl.ds(base, rows), :] = acc
        return carry

    lax.fori_loop(0, tm // rows, conv_rows, 0)
    yd = _silu(_layernorm(dbuf[...] + confb_ref[...], clng_ref[...], clnb_ref[...]))

    mix = _merge_branches(h16, (ya, yb, yc, yd), win_ref, wbr_ref, wout_ref)
    xo_ref[...] = x + gate1 * mix

    cbuf[0:CONV_HIST, :] = cbuf[tm:tm + CONV_HIST, :]
    pbuf[0:POOL_HIST, :] = pbuf[tm:tm + POOL_HIST, :]
    gbuf[0:CONF_HIST, :] = gbuf[tm:tm + CONF_HIST, :]

    @pl.when(t == last_t)
    def _():
        cst_ref[...] = cbuf[CONV_HIST - (CONV_B_WIDTH - 1):CONV_HIST, :]
        pst_ref[...] = pbuf[POOL_HIST - POOL_STATE:POOL_HIST, :]
        fst_ref[...] = gbuf[CONF_HIST - (CONF_WIDTH - 1):CONF_HIST, :]


def _layer_spec(shape):
    return (None,) + tuple(shape)


def _mixer_weight_specs(l, n_grid):
    zeros = (0,) * 8

    def spec(shape, single_buffer=False):
        nd = len(shape)
        idx = (lambda *g: (l,) + zeros[:nd])
        if single_buffer:
            return pl.BlockSpec(_layer_spec(shape), idx, pipeline_mode=pl.Buffered(1))
        return pl.BlockSpec(_layer_spec(shape), idx)

    del n_grid
    return spec


def _mixer_prompt(l, x, mod_p, w, tm=512):
    bsz, seq, _ = x.shape
    spec = _mixer_weight_specs(l, 2)
    kern = functools.partial(_mixer_prompt_kernel, tm=tm)
    in_specs = [
        pl.BlockSpec((None, tm, D_MODEL), lambda b, t: (b, t, 0)),
        pl.BlockSpec((None, None, 6, D_MODEL), lambda b, t: (l, b, 0, 0)),
        spec((1, D_MODEL)),
        spec((D_MODEL, IN_COLS), True),
        spec((1, BR_WIDTH)), spec((1, BR_WIDTH)),
        spec((N_GROUPS, CHUNK, CHUNK)),
        spec((N_GROUPS, CHUNK, 1)),
        spec((CONV_B_WIDTH, BR_WIDTH)),
        spec((N_GROUPS, GROUP_CH, GROUP_CH)),
        spec((1, BR_WIDTH)),
        spec((CONF_WIDTH, BR_WIDTH)),
        spec((1, BR_WIDTH)), spec((1, BR_WIDTH)), spec((1, BR_WIDTH)),
        spec((4, BR_WIDTH, D_MODEL), True),
        spec((D_MODEL, D_MODEL), True),
    ]
    out_specs = [
        pl.BlockSpec((None, tm, D_MODEL), lambda b, t: (b, t, 0)),
        pl.BlockSpec((None, CONV_B_WIDTH - 1, BR_WIDTH), lambda b, t: (b, 0, 0)),
        pl.BlockSpec((None, POOL_STATE, BR_WIDTH), lambda b, t: (b, 0, 0)),
        pl.BlockSpec((None, CONF_WIDTH - 1, BR_WIDTH), lambda b, t: (b, 0, 0)),
    ]
    out_shape = [
        jax.ShapeDtypeStruct((bsz, seq, D_MODEL), F32),
        jax.ShapeDtypeStruct((bsz, CONV_B_WIDTH - 1, BR_WIDTH), F32),
        jax.ShapeDtypeStruct((bsz, POOL_STATE, BR_WIDTH), F32),
        jax.ShapeDtypeStruct((bsz, CONF_WIDTH - 1, BR_WIDTH), F32),
    ]
    return pl.pallas_call(
        kern,
        grid=(bsz, seq // tm),
        in_specs=in_specs,
        out_specs=out_specs,
        out_shape=out_shape,
        scratch_shapes=[
            pltpu.VMEM((CONV_HIST + tm, BR_WIDTH), F32),
            pltpu.VMEM((POOL_HIST + tm, BR_WIDTH), F32),
            pltpu.VMEM((CONF_HIST + tm, BR_WIDTH), F32),
            pltpu.VMEM((tm, BR_WIDTH), F32),
            pltpu.VMEM((SUBLANES - 1, tm + CONF_HIST - SUBLANES, BR_WIDTH), F32),
        ],
        compiler_params=pltpu.CompilerParams(
            dimension_semantics=("arbitrary", "arbitrary"), vmem_limit_bytes=VMEM_LIMIT_BYTES),
        name="mixer_prompt",
    )(x, mod_p, w["norm1_g"], w["w_in"], w["sgu_ln_g"], w["sgu_ln_b"], w["sgu_w"], w["sgu_b"],
      w["conv_w"], w["pool_w"], w["pool_scale"], w["conf_w"], w["conf_b"], w["conf_ln_g"],
      w["conf_ln_b"], w["w_branch"], w["w_out"])


def _mixer_sample_kernel(x_ref, mod_ref, n1g_ref, win_ref, lng_ref, lnb_ref, sguw0_ref, sgub0_ref,
                         convw_ref, poolw_ref, pscale_ref, confw_ref, confb_ref, clng_ref, clnb_ref,
                         wbr_ref, wout_ref, cprev_ref, pprev_ref, fprev_ref,
                         xo_ref, cst_ref, pst_ref, fst_ref, v_ref):
    x = x_ref[...]
    shift1 = mod_ref[0]
    scale1 = mod_ref[1]
    gate1 = mod_ref[2]
    h16 = _rms_mod(x, n1g_ref[...], scale1, shift1).astype(BF16)

    za = jnp.dot(h16, win_ref[:, COL_A:COL_A + 2 * BR_WIDTH], preferred_element_type=F32)
    ga = _gelu(za)
    v = _layernorm(ga[:, BR_WIDTH:], lng_ref[...], lnb_ref[...])
    v_ref[...] = v
    ya = ga[:, :BR_WIDTH] * (sguw0_ref[...] * v + sgub0_ref[...])

    zb = jnp.dot(h16, win_ref[:, COL_B:COL_B + 3 * BR_WIDTH], preferred_element_type=F32)
    ch = zb[:, BR_WIDTH:2 * BR_WIDTH] * zb[:, 2 * BR_WIDTH:]
    conv = convw_ref[0:1, :] * cprev_ref[0] + convw_ref[1:2, :] * cprev_ref[1] + convw_ref[2:3, :] * ch
    yb = zb[:, :BR_WIDTH] * conv
    cst_ref[0] = cprev_ref[1]
    cst_ref[1] = ch

    p = jnp.dot(h16, win_ref[:, COL_C:COL_C + BR_WIDTH], preferred_element_type=F32)
    yc_cols = []
    for g, win in enumerate(POOL_WINDOWS):
        lanes = slice(g * GROUP_CH, (g + 1) * GROUP_CH)
        wsum = p[:, lanes]
        for i in range(1, win):
            wsum = wsum + pprev_ref[POOL_STATE - i, :, lanes]
        pooled = wsum / float(win) - p[:, lanes]
        yc_cols.append(_dot(pooled, poolw_ref[g]))
    yc = jnp.concatenate(yc_cols, axis=1) * pscale_ref[...]
    pst_ref[0:POOL_STATE - 1] = pprev_ref[1:POOL_STATE]
    pst_ref[POOL_STATE - 1] = p

    zd = jnp.dot(h16, win_ref[:, COL_D:COL_D + 2 * BR_WIDTH], preferred_element_type=F32)
    glu = zd[:, :BR_WIDTH] * _sigmoid(zd[:, BR_WIDTH:])
    dconv = confw_ref[CONF_WIDTH - 1:CONF_WIDTH, :] * glu
    for j in range(CONF_WIDTH - 1):
        dconv = dconv + confw_ref[j:j + 1, :] * fprev_ref[j]
    yd = _silu(_layernorm(dconv + confb_ref[...], clng_ref[...], clnb_ref[...]))
    fst_ref[0:CONF_WIDTH - 2] = fprev_ref[1:CONF_WIDTH - 1]
    fst_ref[CONF_WIDTH - 2] = glu

    mix = _merge_branches(h16, (ya, yb, yc, yd), win_ref, wbr_ref, wout_ref)
    xo_ref[...] = x + gate1 * mix


def _mixer_sample(l, x, mod_s, w, cprev, pprev, fprev, tr=64):
    rows = x.shape[0]
    spec = _mixer_weight_specs(l, 1)

    def state_in(k):
        return pl.BlockSpec((None, k, tr, BR_WIDTH), lambda i: (l, 0, i, 0))

    def state_out(k):
        return pl.BlockSpec((k, tr, BR_WIDTH), lambda i: (0, i, 0))

    in_specs = [
        pl.BlockSpec((tr, D_MODEL), lambda i: (i, 0)),
        pl.BlockSpec((None, 6, tr, D_MODEL), lambda i: (l, 0, i, 0)),
        spec((1, D_MODEL)),
        spec((D_MODEL, IN_COLS), True),
        spec((1, BR_WIDTH)), spec((1, BR_WIDTH)),
        spec((1, BR_WIDTH)), spec((1, BR_WIDTH)),
        spec((CONV_B_WIDTH, BR_WIDTH)),
        spec((N_GROUPS, GROUP_CH, GROUP_CH)),
        spec((1, BR_WIDTH)),
        spec((CONF_WIDTH, BR_WIDTH)),
        spec((1, BR_WIDTH)), spec((1, BR_WIDTH)), spec((1, BR_WIDTH)),
        spec((4, BR_WIDTH, D_MODEL), True),
        spec((D_MODEL, D_MODEL), True),
        state_in(CONV_B_WIDTH - 1), state_in(POOL_STATE), state_in(CONF_WIDTH - 1),
    ]
    out_shape = [
        jax.ShapeDtypeStruct((rows, D_MODEL), F32),
        jax.ShapeDtypeStruct((CONV_B_WIDTH - 1, rows, BR_WIDTH), F32),
        jax.ShapeDtypeStruct((POOL_STATE, rows, BR_WIDTH), F32),
        jax.ShapeDtypeStruct((CONF_WIDTH - 1, rows, BR_WIDTH), F32),
        jax.ShapeDtypeStruct((rows, BR_WIDTH), F32),
    ]
    out_specs = [
        pl.BlockSpec((tr, D_MODEL), lambda i: (i, 0)),
        state_out(CONV_B_WIDTH - 1), state_out(POOL_STATE), state_out(CONF_WIDTH - 1),
        pl.BlockSpec((tr, BR_WIDTH), lambda i: (i, 0)),
    ]
    return pl.pallas_call(
        _mixer_sample_kernel,
        grid=(rows // tr,),
        in_specs=in_specs,
        out_specs=out_specs,
        out_shape=out_shape,
        compiler_params=pltpu.CompilerParams(
            dimension_semantics=("arbitrary",), vmem_limit_bytes=VMEM_LIMIT_BYTES),
        name="mixer_sample",
    )(x, mod_s, w["norm1_g"], w["w_in"], w["sgu_ln_g"], w["sgu_ln_b"], w["sgu_w0"], w["sgu_b0"],
      w["conv_w"], w["pool_w"], w["pool_scale"], w["conf_w"], w["conf_b"], w["conf_ln_g"],
      w["conf_ln_b"], w["w_branch"], w["w_out"], cprev, pprev, fprev)


def _reduce_keep(fn, x, axes):
    for a in axes:
        x = fn(x, axis=a, keepdims=True)
    return x


def _first_max(x, idx, axes, size):
    m = _reduce_keep(jnp.max, x, axes)
    first = _reduce_keep(jnp.min, jnp.where(x == m, idx, size), axes)
    return m, idx == first


def _router_kernel(x_ref, mod_ref, n2g_ref, rwt_ref, rb_ref, h_ref, gates_ref, *, per_row_mod):
    x = x_ref[...]
    if per_row_mod:
        shift2, scale2 = mod_ref[3], mod_ref[4]
    else:
        shift2, scale2 = mod_ref[3:4, :], mod_ref[4:5, :]
    h = _rms_mod(x, n2g_ref[...], scale2, shift2)
    h16 = h.astype(BF16)
    h_ref[...] = h16
    logits = lax.dot_general(rwt_ref[...], h16, (((1,), (1,)), ((), ())), preferred_element_type=F32)
    tm = logits.shape[1]
    shape3 = (N_EXPERT_GROUPS, GROUP_SIZE, tm)
    scores = _sigmoid(logits).reshape(shape3)
    biased = scores + rb_ref[...]
    eidx = lax.broadcasted_iota(jnp.int32, shape3, 1)
    flat = lax.broadcasted_iota(jnp.int32, shape3, 0) * GROUP_SIZE + eidx
    m1, first = _first_max(biased, eidx, (1,), GROUP_SIZE)
    m2 = jnp.max(jnp.where(first, -jnp.inf, biased), axis=1, keepdims=True)
    grp_score = m1 + m2
    gidx = lax.broadcasted_iota(jnp.int32, grp_score.shape, 0)
    penalty = jnp.full(grp_score.shape, -jnp.inf, F32)
    for _ in range(TOPK_GROUPS):
        _, hit = _first_max(grp_score, gidx, (0,), N_EXPERT_GROUPS)
        penalty = jnp.where(hit, 0.0, penalty)
        grp_score = jnp.where(hit, -jnp.inf, grp_score)
    cand = biased + penalty
    sel = jnp.zeros(shape3, F32)
    for _ in range(TOP_K):
        _, hit = _first_max(cand, flat, (0, 1), N_EXPERTS)
        sel = jnp.where(hit, 1.0, sel)
        cand = jnp.where(hit, -jnp.inf, cand)
    wsel = sel * scores
    gates = wsel / _reduce_keep(jnp.sum, wsel, (0, 1)) * ROUTED_SCALE
    gates_ref[...] = gates.reshape(N_EXPERTS, tm).T


def _router(l, x, mod, w, *, tm, per_row_mod):
    n = x.shape[0]
    kern = functools.partial(_router_kernel, per_row_mod=per_row_mod)
    if per_row_mod:
        mod_spec = pl.BlockSpec((None, 6, tm, D_MODEL), lambda i: (l, 0, i, 0))
    else:
        tiles_per_seq = mod["tiles_per_seq"]
        mod_spec = pl.BlockSpec((None, None, 6, D_MODEL), lambda i: (l, i // tiles_per_seq, 0, 0))
    return pl.pallas_call(
        kern,
        grid=(n // tm,),
        in_specs=[
            pl.BlockSpec((tm, D_MODEL), lambda i: (i, 0)),
            mod_spec,
            pl.BlockSpec((None, 1, D_MODEL), lambda i: (l, 0, 0)),
            pl.BlockSpec((None, N_EXPERTS, D_MODEL), lambda i: (l, 0, 0)),
            pl.BlockSpec((None, N_EXPERT_GROUPS, GROUP_SIZE, 1), lambda i: (l, 0, 0, 0)),
        ],
        out_specs=[
            pl.BlockSpec((tm, D_MODEL), lambda i: (i, 0)),
            pl.BlockSpec((tm, N_EXPERTS), lambda i: (i, 0)),
        ],
        out_shape=[
            jax.ShapeDtypeStruct((n, D_MODEL), BF16),
            jax.ShapeDtypeStruct((n, N_EXPERTS), F32),
        ],
        compiler_params=pltpu.CompilerParams(
            dimension_semantics=("arbitrary",), vmem_limit_bytes=VMEM_LIMIT_BYTES),
        name="router",
    )(x, mod["array"], w["norm2_g"], w["router_wt"], w["router_bias"])


def _experts_kernel(x_ref, h_ref, gates_ref, mod_ref, eg_ref, eu_ref, ed_ref, sg_ref, su_ref, sd_ref,
                    fg_ref, xo_ref, acc_ref, *, per_row_mod, final_norm):
    e = pl.program_id(1)
    h16 = h_ref[...]

    @pl.when(e == 0)
    def _():
        shared = _silu(jnp.dot(h16, sg_ref[...], preferred_element_type=F32)) * jnp.dot(
            h16, su_ref[...], preferred_element_type=F32)
        acc_ref[...] = _dot(shared, sd_ref[...])

    lane = lax.broadcasted_iota(jnp.int32, gates_ref.shape, 1)
    gate_e = jnp.sum(jnp.where(lane == e, gates_ref[...], 0.0), axis=1, keepdims=True)
    hg = jnp.dot(h16, eg_ref[...], preferred_element_type=F32)
    hu = jnp.dot(h16, eu_ref[...], preferred_element_type=F32)
    act = _silu(hg) * hu * gate_e
    acc_ref[...] += _dot(act, ed_ref[...])

    @pl.when(e == pl.num_programs(1) - 1)
    def _():
        gate2 = mod_ref[5] if per_row_mod else mod_ref[5:6, :]
        y = x_ref[...] + gate2 * acc_ref[...]
        if final_norm:
            ms = jnp.mean(y * y, axis=-1, keepdims=True)
            y = y * lax.rsqrt(ms + EPS) * fg_ref[...]
        xo_ref[...] = y


def _experts(l, x, h16, gates, mod, w, final_g, *, tm, per_row_mod, final_norm):
    n = x.shape[0]
    kern = functools.partial(_experts_kernel, per_row_mod=per_row_mod, final_norm=final_norm)
    if per_row_mod:
        mod_spec = pl.BlockSpec((None, 6, tm, D_MODEL), lambda i, e: (l, 0, i, 0))
    else:
        tiles_per_seq = mod["tiles_per_seq"]
        mod_spec = pl.BlockSpec((None, None, 6, D_MODEL), lambda i, e: (l, i // tiles_per_seq, 0, 0))
    return pl.pallas_call(
        kern,
        grid=(n // tm, N_EXPERTS),
        in_specs=[
            pl.BlockSpec((tm, D_MODEL), lambda i, e: (i, 0)),
            pl.BlockSpec((tm, D_MODEL), lambda i, e: (i, 0)),
            pl.BlockSpec((tm, N_EXPERTS), lambda i, e: (i, 0)),
            mod_spec,
            pl.BlockSpec((None, None, D_MODEL, D_EXPERT), lambda i, e: (l, e, 0, 0)),
            pl.BlockSpec((None, None, D_MODEL, D_EXPERT), lambda i, e: (l, e, 0, 0)),
            pl.BlockSpec((None, None, D_EXPERT, D_MODEL), lambda i, e: (l, e, 0, 0)),
            pl.BlockSpec((None, D_MODEL, D_EXPERT), lambda i, e: (l, 0, 0)),
            pl.BlockSpec((None, D_MODEL, D_EXPERT), lambda i, e: (l, 0, 0)),
            pl.BlockSpec((None, D_EXPERT, D_MODEL), lambda i, e: (l, 0, 0)),
            pl.BlockSpec((1, D_MODEL), lambda i, e: (0, 0)),
        ],
        out_specs=pl.BlockSpec((tm, D_MODEL), lambda i, e: (i, 0)),
        out_shape=jax.ShapeDtypeStruct((n, D_MODEL), F32),
        scratch_shapes=[pltpu.VMEM((tm, D_MODEL), F32)],
        compiler_params=pltpu.CompilerParams(
            dimension_semantics=("arbitrary", "arbitrary"), vmem_limit_bytes=VMEM_LIMIT_BYTES),
        name="experts",
    )(x, h16, gates, mod["array"], w["e_gate"], w["e_up"], w["e_down"], w["s_gate"], w["s_up"],
      w["s_down"], final_g)


def kernel(x_prompt, x_sample, state_conv, state_pool, state_conf, c_prompt, c_sample, ada_w, ada_b, norm1_g, norm2_g, w_in, sgu_ln_g, sgu_ln_b, sgu_w, sgu_b, conv_w, pool_w, pool_scale, conf_w, conf_b, conf_ln_g, conf_ln_b, w_branch, w_out, router_w, router_bias, e_gate, e_up, e_down, s_gate, s_up, s_down, final_g):
    n_layers = ada_w.shape[0]
    bsz, seq, _ = x_prompt.shape
    n_samp = x_sample.shape[0]

    def row(a):
        return a.reshape(n_layers, 1, a.shape[-1])

    weights = {
        "norm1_g": row(norm1_g), "norm2_g": row(norm2_g),
        "w_in": w_in.astype(BF16),
        "sgu_ln_g": row(sgu_ln_g), "sgu_ln_b": row(sgu_ln_b),
        "sgu_w": sgu_w, "sgu_b": sgu_b[..., None],
        "sgu_w0": row(jnp.repeat(sgu_w[:, :, 0, 0], GROUP_CH, axis=-1)),
        "sgu_b0": row(jnp.repeat(sgu_b[:, :, 0], GROUP_CH, axis=-1)),
        "conv_w": conv_w, "pool_w": pool_w.astype(BF16), "pool_scale": row(pool_scale),
        "conf_w": conf_w, "conf_b": row(conf_b), "conf_ln_g": row(conf_ln_g), "conf_ln_b": row(conf_ln_b),
        "w_branch": w_branch.astype(BF16), "w_out": w_out.astype(BF16),
        "router_wt": jnp.swapaxes(router_w, 1, 2).astype(BF16),
        "router_bias": router_bias.reshape(n_layers, N_EXPERT_GROUPS, GROUP_SIZE, 1),
        "e_gate": e_gate.astype(BF16), "e_up": e_up.astype(BF16), "e_down": e_down.astype(BF16),
        "s_gate": s_gate.astype(BF16), "s_up": s_up.astype(BF16), "s_down": s_down.astype(BF16),
    }
    final_g2 = final_g.reshape(1, D_MODEL)

    mod = _modulation(jnp.concatenate([c_prompt, c_sample], axis=0), ada_w, ada_b)
    mod = mod.reshape(n_layers, bsz + n_samp, 6, D_MODEL)
    mod_p = mod[:, :bsz]
    mod_s = jnp.swapaxes(mod[:, bsz:], 1, 2)

    conv_t = jnp.swapaxes(state_conv, 1, 2)
    pool_t = jnp.swapaxes(state_pool, 1, 2)
    conf_t = jnp.swapaxes(state_conf, 1, 2)

    tm_moe = 1024
    mod_p_moe = {"array": mod_p, "tiles_per_seq": seq // tm_moe}
    mod_s_moe = {"array": mod_s}

    xp = x_prompt
    xs = x_sample.reshape(n_samp, D_MODEL)
    conv_p, pool_p, conf_p, conv_s, pool_s, conf_s, chunk_v = [], [], [], [], [], [], []
    for l in range(n_layers):
        last = l == n_layers - 1
        xp, cp, pp, fp = _mixer_prompt(l, xp, mod_p, weights)
        conv_p.append(cp); pool_p.append(pp); conf_p.append(fp)
        xs, cs, ps, fs, vs = _mixer_sample(l, xs, mod_s, weights, conv_t, pool_t, conf_t)
        conv_s.append(cs); pool_s.append(ps); conf_s.append(fs); chunk_v.append(vs)

        xp2 = xp.reshape(bsz * seq, D_MODEL)
        hp, gp = _router(l, xp2, mod_p_moe, weights, tm=tm_moe, per_row_mod=False)
        xp = _experts(l, xp2, hp, gp, mod_p_moe, weights, final_g2, tm=tm_moe, per_row_mod=False,
                      final_norm=last).reshape(bsz, seq, D_MODEL)
        hs, gs = _router(l, xs, mod_s_moe, weights, tm=n_samp, per_row_mod=True)
        xs = _experts(l, xs, hs, gs, mod_s_moe, weights, final_g2, tm=n_samp, per_row_mod=True,
                      final_norm=last)

    def samp_state(parts):
        return jnp.stack([jnp.swapaxes(s, 0, 1) for s in parts])

    return (xp, xs.reshape(n_samp, 1, D_MODEL),
            jnp.stack(conv_p), jnp.stack(pool_p), jnp.stack(conf_p),
            samp_state(conv_s), samp_state(pool_s), samp_state(conf_s),
            jnp.stack(chunk_v)[:, :, None, :])
```
